```python
import math
import jax, jax.numpy as jnp
from jax import lax
import numpy as np

D_MODEL = 1024
BATCH = 4
SEQ = 4096
DEPTH = 1

GRID_W = 64
CTX_LEN = 256
EPS = 1e-6

GLA_HEADS = 4
GLA_DK = D_MODEL // 2
GLA_DV = D_MODEL
GLA_HK = GLA_DK // GLA_HEADS
GLA_HV = GLA_DV // GLA_HEADS
GLA_RANK = 16
GLA_NORMALIZER = 16.0
GLA_CHUNK = 64

SSD_DINNER = 2 * D_MODEL
SSD_HEADDIM = 64
SSD_HEADS = SSD_DINNER // SSD_HEADDIM
SSD_GROUPS = 8
SSD_HPG = SSD_HEADS // SSD_GROUPS
SSD_STATE = 128
SSD_CONV = 5
SSD_CHUNK = 128
SSD_XBC = SSD_DINNER + 2 * SSD_GROUPS * SSD_STATE
DT_MIN = 1e-3
DT_MAX = 1e-1

N_EXPERTS = 256
TOP_K = 8
N_EXPERT_GROUPS = 8
TOPK_GROUPS = 4
EXPERT_FF = 256
SHARED_FF = 256
ROUTED_SCALE = 2.5
MOE_BLOCK = 128

N_BRANCH = 2
DEEPNORM_ALPHA = (2.0 * DEPTH) ** 0.25
DEEPNORM_BETA = (8.0 * DEPTH) ** -0.25

IN_SIZES = (GLA_DK, GLA_DK, GLA_DV, GLA_DV, GLA_RANK, GLA_RANK, SSD_DINNER, SSD_XBC, SSD_HEADS, SSD_HEADS, N_BRANCH * D_MODEL)
IN_TOTAL = sum(IN_SIZES)

kernel_name = "hybrid_gla_ssd_moe_dit_block"


def _split_cols(h, sizes):
    points, acc = [], 0
    for s in sizes[:-1]:
        acc += s
        points.append(acc)
    return jnp.split(h, points, axis=-1)


def _layernorm(x):
    xf = x.astype(jnp.float32)
    mu = jnp.mean(xf, axis=-1, keepdims=True)
    var = jnp.mean(jnp.square(xf - mu), axis=-1, keepdims=True)
    return ((xf - mu) * lax.rsqrt(var + EPS)).astype(x.dtype)


def _layernorm_affine(x, g, b):
    return _layernorm(x) * g + b


def _rmsnorm(x, w):
    xf = x.astype(jnp.float32)
    y = xf * lax.rsqrt(jnp.mean(xf * xf, axis=-1, keepdims=True) + EPS)
    return (y * w).astype(x.dtype)


def _modulate(x, shift, scale):
    return _layernorm(x) * (1.0 + scale) + shift


def _post_norm(x, delta, g, b):
    return _layernorm_affine(DEEPNORM_ALPHA * x + delta, g, b)


def _flip(a):
    return jnp.flip(a, axis=1)


def _gla_scan(q, k, v, g, s0):
    b, t, h, _ = q.shape
    dv = v.shape[-1]
    out_dtype = v.dtype
    L = GLA_CHUNK
    nc = t // L
    q, k, v, g = (a.astype(jnp.float32).reshape(b, nc, L, h, -1) for a in (q, k, v, g))
    G = jnp.cumsum(g, axis=2)
    G_last = G[:, :, -1]
    G_mid = G[:, :, L // 2 - 1:L // 2]
    A = jnp.einsum('bclhd,bcmhd->bchlm', q * jnp.exp(G - G_mid), k * jnp.exp(G_mid - G))
    A = jnp.where(jnp.tril(jnp.ones((L, L), bool)), A, 0.0)
    o_intra = jnp.einsum('bchlm,bcmhv->bclhv', A, v)
    dS = jnp.einsum('bclhd,bclhv->bchdv', k * jnp.exp(G_last[:, :, None] - G), v)
    decay = jnp.exp(G_last)

    def step(S, inp):
        dec, ds = inp
        return dec[..., None] * S + ds, S

    S_fin, S_starts = lax.scan(step, s0.astype(jnp.float32),
                               (jnp.moveaxis(decay, 1, 0), jnp.moveaxis(dS, 1, 0)))
    S_starts = jnp.moveaxis(S_starts, 0, 1)
    o_inter = jnp.einsum('bclhd,bchdv->bclhv', q * jnp.exp(G), S_starts)
    return (o_intra + o_inter).reshape(b, t, h, dv).astype(out_dtype), S_fin


def _ssd_scan(x, dt, a, bm, cm, s0):
    b, t, _, p = x.shape
    out_dtype = x.dtype
    L = SSD_CHUNK
    nc = t // L
    f32 = jnp.float32
    x = x.astype(f32).reshape(b, nc, L, SSD_GROUPS, SSD_HPG, p)
    dt = dt.astype(f32).reshape(b, nc, L, SSD_GROUPS, SSD_HPG)
    bm = bm.astype(f32).reshape(b, nc, L, SSD_GROUPS, SSD_STATE)
    cm = cm.astype(f32).reshape(b, nc, L, SSD_GROUPS, SSD_STATE)
    cum = jnp.cumsum(dt * a.reshape(SSD_GROUPS, SSD_HPG), axis=2)
    seg = cum[:, :, :, None] - cum[:, :, None]
    mask = jnp.tril(jnp.ones((L, L), bool))[:, :, None, None]
    Lmat = jnp.exp(jnp.where(mask, seg, -jnp.inf))
    cb = jnp.einsum('bclgn,bcmgn->bclmg', cm, bm)
    wts = cb[..., None] * Lmat * dt[:, :, None]
    y_intra = jnp.einsum('bclmge,bcmgep->bclgep', wts, x)
    cum_last = cum[:, :, -1]
    xw = x * (jnp.exp(cum_last[:, :, None] - cum) * dt)[..., None]
    dS = jnp.einsum('bclgn,bclgep->bcgepn', bm, xw)
    decay = jnp.exp(cum_last)

    def step(S, inp):
        dec, ds = inp
        return dec[..., None, None] * S + ds, S

    S_fin, S_starts = lax.scan(step, s0.astype(f32),
                               (jnp.moveaxis(decay, 1, 0), jnp.moveaxis(dS, 1, 0)))
    S_starts = jnp.moveaxis(S_starts, 0, 1)
    y_inter = jnp.einsum('bclgn,bcgepn->bclgep', cm, S_starts) * jnp.exp(cum)[..., None]
    return (y_intra + y_inter).reshape(b, t, SSD_HEADS, p).astype(out_dtype), S_fin


def _dwconv(x, w, bias, rows):
    bsz, t, ch = x.shape
    xs = x if rows is None else x.reshape(bsz * rows, GRID_W, ch)
    y = lax.conv_general_dilated(xs, w[:, None, :].astype(x.dtype), window_strides=(1,),
                                 padding=[(SSD_CONV // 2, SSD_CONV // 2)],
                                 dimension_numbers=('NWC', 'WIO', 'NWC'), feature_group_count=ch)
    return y.reshape(bsz, t, ch) + bias


def _gla_log_decay(lo, w_up, bias):
    return jax.nn.log_sigmoid((lo @ w_up + bias).astype(jnp.float32)) / GLA_NORMALIZER


def _token_mixer(u, lp, init, rows, need_out):
    bsz, t, _ = u.shape
    f32 = jnp.float32
    h = u @ lp['w_in']
    q, k, v, r, lo_f, lo_b, z, xbc, dt_f, dt_b, gate_logits = _split_cols(h, IN_SIZES)
    if init is None:
        zg = jnp.zeros((bsz, GLA_HEADS, GLA_HK, GLA_HV), f32)
        zs = jnp.zeros((bsz, SSD_GROUPS, SSD_HPG, SSD_HEADDIM, SSD_STATE), f32)
        init = (zg, zg, zs, zs)
    gla_f0, gla_b0, ssd_f0, ssd_b0 = init

    q = q.reshape(bsz, t, GLA_HEADS, GLA_HK) * (GLA_HK ** -0.5)
    k = k.reshape(bsz, t, GLA_HEADS, GLA_HK)
    v = v.reshape(bsz, t, GLA_HEADS, GLA_HV)
    g_f = _gla_log_decay(lo_f, lp['gla_a_up'][0], lp['gla_a_bias'][0]).reshape(bsz, t, GLA_HEADS, GLA_HK)
    g_b = _gla_log_decay(lo_b, lp['gla_a_up'][1], lp['gla_a_bias'][1]).reshape(bsz, t, GLA_HEADS, GLA_HK)
    o_f, gla_f1 = _gla_scan(q, k, v, g_f, gla_f0)
    o_b, gla_b1 = _gla_scan(_flip(q), _flip(k), _flip(v), _flip(g_b), gla_b0)

    xbc = jax.nn.silu(_dwconv(xbc, lp['ssd_conv_w'], lp['ssd_conv_b'], rows))
    xs, bm, cm = _split_cols(xbc, (SSD_DINNER, SSD_GROUPS * SSD_STATE, SSD_GROUPS * SSD_STATE))
    xs = xs.reshape(bsz, t, SSD_HEADS, SSD_HEADDIM)
    bm = bm.reshape(bsz, t, SSD_GROUPS, SSD_STATE)
    cm = cm.reshape(bsz, t, SSD_GROUPS, SSD_STATE)
    dtf = jax.nn.softplus((dt_f + lp['ssd_dt_bias'][0]).astype(f32))
    dtb = jax.nn.softplus((dt_b + lp['ssd_dt_bias'][1]).astype(f32))
    a_f = -jnp.exp(lp['ssd_a_log'][0].astype(f32))
    a_b = -jnp.exp(lp['ssd_a_log'][1].astype(f32))
    y_f, ssd_f1 = _ssd_scan(xs, dtf, a_f, bm, cm, ssd_f0)
    y_b, ssd_b1 = _ssd_scan(_flip(xs), _flip(dtb), a_b, _flip(bm), _flip(cm), ssd_b0)
    states = (gla_f1, gla_b1, ssd_f1, ssd_b1)
    if not need_out:
        return None, states

    o = _rmsnorm(o_f + _flip(o_b), lp['gla_norm_w']).reshape(bsz, t, GLA_DV)
    y_gla = (o * jax.nn.silu(r)) @ lp['w_branch_gla']
    y = (y_f + _flip(y_b) + xs * lp['ssd_d'][:, None]).reshape(bsz, t, SSD_DINNER) * jax.nn.silu(z)
    y = _rmsnorm(y.reshape(bsz, t, SSD_GROUPS, -1),
                 lp['ssd_norm_w'].reshape(SSD_GROUPS, -1)).reshape(bsz, t, SSD_DINNER)
    y_ssd = y @ lp['w_branch_ssd']
    gate_gla, gate_ssd = jnp.split(jax.nn.sigmoid(gate_logits), 2, axis=-1)
    merged = gate_gla * y_gla + gate_ssd * y_ssd
    return merged @ lp['w_out'], states


def _swiglu(x, wg, wu, wd):
    return (jax.nn.silu(x @ wg) * (x @ wu)) @ wd


def _moe(u, lp):
    nt, d = u.shape
    scores = jax.nn.sigmoid((u @ lp['router_w']).astype(jnp.float32))
    biased = scores + lp['router_bias'].astype(jnp.float32)
    grp_score = lax.top_k(biased.reshape(nt, N_EXPERT_GROUPS, -1), 2)[0].sum(-1)
    _, top_groups = lax.top_k(grp_score, TOPK_GROUPS)
    gmask = jnp.any(top_groups[:, :, None] == jnp.arange(N_EXPERT_GROUPS)[None, None, :], axis=1)
    masked = jnp.where(jnp.repeat(gmask, N_EXPERTS // N_EXPERT_GROUPS, axis=1), biased, -jnp.inf)
    _, idx = lax.top_k(masked, TOP_K)
    w = jnp.take_along_axis(scores, idx, axis=1)
    w = w / jnp.sum(w, axis=-1, keepdims=True) * ROUTED_SCALE

    n_assign = nt * TOP_K
    flat_e = idx.reshape(-1).astype(jnp.int32)
    flat_tok = jnp.repeat(jnp.arange(nt, dtype=jnp.int32), TOP_K)
    se, stok, sw = lax.sort((flat_e, flat_tok, w.reshape(-1)), num_keys=1)
    counts = jnp.zeros((N_EXPERTS,), jnp.int32).at[flat_e].add(1)
    starts = jnp.cumsum(counts) - counts
    padded = (counts + MOE_BLOCK - 1) // MOE_BLOCK * MOE_BLOCK
    pends = jnp.cumsum(padded)
    pstarts = pends - padded
    dest = pstarts[se] + (jnp.arange(n_assign, dtype=jnp.int32) - starts[se])
    n_blocks = -(-n_assign // MOE_BLOCK) + N_EXPERTS
    n_rows = n_blocks * MOE_BLOCK
    buf_tok = jnp.full((n_rows,), nt, jnp.int32).at[dest].set(stok)
    buf_w = jnp.zeros((n_rows,), u.dtype).at[dest].set(sw.astype(u.dtype))
    block_start = jnp.arange(n_blocks, dtype=jnp.int32) * MOE_BLOCK
    block_e = jnp.minimum(jnp.searchsorted(pends, block_start, side='right'), N_EXPERTS - 1)
    u_pad = jnp.concatenate([u, jnp.zeros((1, d), u.dtype)], axis=0)

    def body(acc, blk):
        e, tok, wt = blk
        yb = _swiglu(u_pad[tok], lp['exp_w_gate'][e], lp['exp_w_up'][e], lp['exp_w_down'][e])
        return acc.at[tok].add(yb * wt[:, None]), None

    acc, _ = lax.scan(body, jnp.zeros((nt + 1, d), u.dtype),
                      (block_e, buf_tok.reshape(n_blocks, MOE_BLOCK), buf_w.reshape(n_blocks, MOE_BLOCK)))
    shared = _swiglu(u, lp['sh_w_gate'], lp['sh_w_up'], lp['sh_w_down'])
    return acc[:nt] + shared


def _moe_sublayer(x, shift, scale, gate, lp):
    bsz, t, d = x.shape
    y = _moe(_modulate(x, shift, scale).reshape(bsz * t, d), lp).reshape(bsz, t, d)
    return _post_norm(x, gate * y, lp['ln2_g'], lp['ln2_b'])


def setup_inputs(seed: int = 0) -> dict:
    key = jax.random.key(seed)
    keys = iter(jax.random.split(key, 40))
    f32 = jnp.float32
    D = D_MODEL

    def nrm(shape, scale):
        return jax.random.normal(next(keys), shape, f32) * scale

    def gain(shape):
        return 1.0 + nrm(shape, 0.01)

    dt0 = jnp.exp(jax.random.uniform(next(keys), (DEPTH, 2, SSD_HEADS), f32, math.log(DT_MIN), math.log(DT_MAX)))
    a0 = jax.random.uniform(next(keys), (DEPTH, 2, SSD_HEADS), f32, 1.0, 16.0)
    return {
        'x': nrm((BATCH, SEQ, D), 1.0),
        'c': nrm((BATCH, D), 1.0),
        'ctx': nrm((BATCH, CTX_LEN, D), 1.0),
        'c_ctx': nrm((D,), 1.0),
        'w_mod': nrm((DEPTH, D, 6 * D), D ** -0.5),
        'b_mod': nrm((DEPTH, 6 * D), 0.01),
        'w_in': nrm((DEPTH, D, IN_TOTAL), D ** -0.5),
        'gla_a_up': nrm((DEPTH, 2, GLA_RANK, GLA_DK), GLA_RANK ** -0.5),
        'gla_a_bias': nrm((DEPTH, 2, GLA_DK), 0.01),
        'gla_norm_w': gain((DEPTH, GLA_HV)),
        'ssd_conv_w': nrm((DEPTH, SSD_CONV, SSD_XBC), SSD_CONV ** -0.5),
        'ssd_conv_b': nrm((DEPTH, SSD_XBC), 0.01),
        'ssd_dt_bias': dt0 + jnp.log(-jnp.expm1(-dt0)),
        'ssd_a_log': jnp.log(a0),
        'ssd_d': gain((DEPTH, SSD_HEADS)),
        'ssd_norm_w': gain((DEPTH, SSD_DINNER)),
        'w_branch_gla': nrm((DEPTH, GLA_DV, D), DEEPNORM_BETA * GLA_DV ** -0.5),
        'w_branch_ssd': nrm((DEPTH, SSD_DINNER, D), DEEPNORM_BETA * SSD_DINNER ** -0.5),
        'w_out': nrm((DEPTH, D, D), DEEPNORM_BETA * D ** -0.5),
        'ln1_g': gain((DEPTH, D)),
        'ln1_b': nrm((DEPTH, D), 0.01),
        'ln2_g': gain((DEPTH, D)),
        'ln2_b': nrm((DEPTH, D), 0.01),
        'router_w': nrm((DEPTH, D, N_EXPERTS), D ** -0.5),
        'router_bias': nrm((DEPTH, N_EXPERTS), 0.01),
        'exp_w_gate': nrm((DEPTH, N_EXPERTS, D, EXPERT_FF), D ** -0.5),
        'exp_w_up': nrm((DEPTH, N_EXPERTS, D, EXPERT_FF), D ** -0.5),
        'exp_w_down': nrm((DEPTH, N_EXPERTS, EXPERT_FF, D), DEEPNORM_BETA * EXPERT_FF ** -0.5),
        'sh_w_gate': nrm((DEPTH, D, SHARED_FF), D ** -0.5),
        'sh_w_up': nrm((DEPTH, D, SHARED_FF), D ** -0.5),
        'sh_w_down': nrm((DEPTH, SHARED_FF, D), DEEPNORM_BETA * SHARED_FF ** -0.5),
    }


def reference(x, c, ctx, c_ctx, w_mod, b_mod, w_in, gla_a_up, gla_a_bias, gla_norm_w,
              ssd_conv_w, ssd_conv_b, ssd_dt_bias, ssd_a_log, ssd_d, ssd_norm_w,
              w_branch_gla, w_branch_ssd, w_out, ln1_g, ln1_b, ln2_g, ln2_b,
              router_w, router_bias, exp_w_gate, exp_w_up, exp_w_down,
              sh_w_gate, sh_w_up, sh_w_down):
    rows = x.shape[1] // GRID_W
    xl, xc = x, ctx
    for l in range(DEPTH):
        last = l == DEPTH - 1
        lp = {
            'w_in': w_in[l], 'gla_a_up': gla_a_up[l], 'gla_a_bias': gla_a_bias[l], 'gla_norm_w': gla_norm_w[l],
            'ssd_conv_w': ssd_conv_w[l], 'ssd_conv_b': ssd_conv_b[l], 'ssd_dt_bias': ssd_dt_bias[l],
            'ssd_a_log': ssd_a_log[l], 'ssd_d': ssd_d[l], 'ssd_norm_w': ssd_norm_w[l],
            'w_branch_gla': w_branch_gla[l], 'w_branch_ssd': w_branch_ssd[l], 'w_out': w_out[l],
            'ln1_g': ln1_g[l], 'ln1_b': ln1_b[l], 'ln2_g': ln2_g[l], 'ln2_b': ln2_b[l],
            'router_w': router_w[l], 'router_bias': router_bias[l],
            'exp_w_gate': exp_w_gate[l], 'exp_w_up': exp_w_up[l], 'exp_w_down': exp_w_down[l],
            'sh_w_gate': sh_w_gate[l], 'sh_w_up': sh_w_up[l], 'sh_w_down': sh_w_down[l],
        }
        mod_l = jax.nn.silu(c) @ w_mod[l] + b_mod[l]
        mod_c = jax.nn.silu(c_ctx) @ w_mod[l] + b_mod[l]
        sh1_l, sc1_l, g1_l, sh2_l, sc2_l, g2_l = (m[:, None, :] for m in jnp.split(mod_l, 6, axis=-1))
        sh1_c, sc1_c, g1_c, sh2_c, sc2_c, g2_c = jnp.split(mod_c, 6, axis=-1)

        out_c, ctx_states = _token_mixer(_modulate(xc, sh1_c, sc1_c), lp, None, None, not last)
        out_l, _ = _token_mixer(_modulate(xl, sh1_l, sc1_l), lp, ctx_states, rows, True)

        xl = _post_norm(xl, g1_l * out_l, lp['ln1_g'], lp['ln1_b'])
        xl = _moe_sublayer(xl, sh2_l, sc2_l, g2_l, lp)
        if not last:
            xc = _post_norm(xc, g1_c * out_c, lp['ln1_g'], lp['ln1_b'])
            xc = _moe_sublayer(xc, sh2_c, sc2_c, g2_c, lp)
    return xl
```

```python
import functools
import math

import jax
import jax.numpy as jnp
from jax import lax
from jax.experimental import pallas as pl
from jax.experimental.pallas import tpu as pltpu

F32 = jnp.float32
BF16 = jnp.bfloat16

D = 1024
GRID_W = 64
EPS = 1e-6

GLA_HEADS = 4
GLA_DK = D // 2
GLA_DV = D
GLA_HK = GLA_DK // GLA_HEADS
GLA_HV = GLA_DV // GLA_HEADS
GLA_RANK = 16
GLA_NORMALIZER = 16.0
GLA_CHUNK = 64

SSD_DINNER = 2 * D
SSD_HEADDIM = 64
SSD_HEADS = SSD_DINNER // SSD_HEADDIM
SSD_GROUPS = 8
SSD_HPG = SSD_HEADS // SSD_GROUPS
SSD_STATE = 128
SSD_CONV = 5
SSD_CHUNK = 128
SSD_XBC = SSD_DINNER + 2 * SSD_GROUPS * SSD_STATE

N_EXPERTS = 256
TOP_K = 8
N_EXPERT_GROUPS = 8
EXPERTS_PER_GROUP = N_EXPERTS // N_EXPERT_GROUPS
TOPK_GROUPS = 4
EXPERT_FF = 256
SHARED_FF = 256
ROUTED_SCALE = 2.5
MOE_BLOCK = 128

DEPTH = 1
DEEPNORM_ALPHA = (2.0 * DEPTH) ** 0.25

IN_SIZES = (GLA_DK, GLA_DK, GLA_DV, GLA_DV, GLA_RANK, GLA_RANK, SSD_DINNER, SSD_XBC,
            SSD_HEADS, SSD_HEADS, 2 * D)

MAIN_COLS = 11264
PROJ_TN = 1024
N_COL_TILES = MAIN_COLS // PROJ_TN
COL_Z, COL_X, COL_GATES = 0, 1, 2
COL_V, COL_R, COL_B, COL_C = 6, 7, 8, 9
COL_Q, COL_K = 20, 21
SMALL_COLS = 128
SM_LO = (0, GLA_RANK)
SM_DT = (2 * GLA_RANK, 2 * GLA_RANK + SSD_HEADS)

VMEM_LIMIT = 56 * 1024 * 1024


def _cparams(sem):
    return pltpu.CompilerParams(dimension_semantics=sem, vmem_limit_bytes=VMEM_LIMIT)


def _silu(x):
    return x * jax.nn.sigmoid(x)


def _softplus(x):
    return jnp.maximum(x, 0.0) + jnp.log1p(jnp.exp(-jnp.abs(x)))


def _log_sigmoid(x):
    return jnp.minimum(x, 0.0) - jnp.log1p(jnp.exp(-jnp.abs(x)))


def _ln(x):
    mu = jnp.mean(x, axis=-1, keepdims=True)
    xc = x - mu
    var = jnp.mean(xc * xc, axis=-1, keepdims=True)
    return xc * lax.rsqrt(var + EPS)


def _dot(a, b):
    return jnp.dot(a, b, preferred_element_type=F32)


def _dot_nt(a, b):
    return lax.dot_general(a, b, (((1,), (1,)), ((), ())), preferred_element_type=F32)


def _dot_tn(a, b):
    return lax.dot_general(a, b, (((0,), (0,)), ((), ())), preferred_element_type=F32)


def _split_hi_lo(x):
    hi = x.astype(BF16)
    lo = (x - hi.astype(F32)).astype(BF16)
    return hi, lo


def _tri_cumsum(tri, x):
    hi, lo = _split_hi_lo(x)
    return _dot(tri, hi) + _dot(tri, lo)


def _tri_mask(n, reverse):
    r = lax.broadcasted_iota(jnp.int32, (n, n), 0)
    c = lax.broadcasted_iota(jnp.int32, (n, n), 1)
    return (c >= r) if reverse else (c <= r)


def _mod_kernel(c_ref, w_ref, b_ref, o_ref):
    a = _silu(c_ref[...]).astype(BF16)
    o_ref[...] = _dot(a, w_ref[...].astype(BF16)) + b_ref[...]


def _mod_call(cc, w_mod, b_mod):
    n = w_mod.shape[1]
    tn = 512
    return pl.pallas_call(
        _mod_kernel,
        grid=(n // tn,),
        in_specs=[pl.BlockSpec((8, D), lambda j: (0, 0)),
                  pl.BlockSpec((D, tn), lambda j: (0, j)),
                  pl.BlockSpec((1, tn), lambda j: (0, j))],
        out_specs=pl.BlockSpec((8, tn), lambda j: (0, j)),
        out_shape=jax.ShapeDtypeStruct((8, n), F32),
        compiler_params=_cparams(("arbitrary",)),
        name="mod_proj",
    )(cc, w_mod, b_mod.reshape(1, n))


def _inproj_kernel(x_ref, sh_ref, sc_ref, wm_ref, ws_ref, cw_ref, cb_ref, hm_ref, hs_ref, u_scr,
                   *, tm, conv_rows):
    j = pl.program_id(2)

    @pl.when(j == 0)
    def _():
        u = _ln(x_ref[0]) * (1.0 + sc_ref[0]) + sh_ref[0]
        ub = u.astype(BF16)
        u_scr[...] = ub
        hs_ref[0] = _dot(ub, ws_ref[...])

    acc = _dot(u_scr[...], wm_ref[...])
    is_conv = (j == 2) | (j == 3) | (j == 8) | (j == 9)

    @pl.when(jnp.logical_not(is_conv))
    def _():
        hm_ref[0] = acc.astype(BF16)

    @pl.when(is_conv)
    def _():
        pos = lax.broadcasted_iota(jnp.int32, (tm, 1), 0) % conv_rows
        half = SSD_CONV // 2
        y = acc * cw_ref[half:half + 1, :]
        for d in (-2, -1, 1, 2):
            sh = pltpu.roll(acc, (-d) % tm, 0)
            ok = (pos + d >= 0) & (pos + d < conv_rows)
            y = y + jnp.where(ok, sh, 0.0) * cw_ref[half + d:half + d + 1, :]
        y = y + cb_ref[...]
        hm_ref[0] = _silu(y).astype(BF16)


def _inproj_call(x, sh, sc, w_main, w_small, conv_w, conv_b, *, tm, conv_rows):
    b, t, _ = x.shape

    def conv_tile(j):
        return jnp.clip(jnp.where(j < 4, j - 2, j - 6), 0, 3)

    kern = functools.partial(_inproj_kernel, tm=tm, conv_rows=conv_rows)
    return pl.pallas_call(
        kern,
        grid=(b, t // tm, N_COL_TILES),
        in_specs=[pl.BlockSpec((1, tm, D), lambda bi, i, j: (bi, i, 0)),
                  pl.BlockSpec((1, 1, D), lambda bi, i, j: (bi, 0, 0)),
                  pl.BlockSpec((1, 1, D), lambda bi, i, j: (bi, 0, 0)),
                  pl.BlockSpec((D, PROJ_TN), lambda bi, i, j: (0, j)),
                  pl.BlockSpec((D, SMALL_COLS), lambda bi, i, j: (0, 0)),
                  pl.BlockSpec((SSD_CONV, PROJ_TN), lambda bi, i, j: (0, conv_tile(j))),
                  pl.BlockSpec((1, PROJ_TN), lambda bi, i, j: (0, conv_tile(j)))],
        out_specs=[pl.BlockSpec((1, tm, PROJ_TN), lambda bi, i, j: (bi, i, j)),
                   pl.BlockSpec((1, tm, SMALL_COLS), lambda bi, i, j: (bi, i, 0))],
        out_shape=[jax.ShapeDtypeStruct((b, t, MAIN_COLS), BF16),
                   jax.ShapeDtypeStruct((b, t, SMALL_COLS), F32)],
        scratch_shapes=[pltpu.VMEM((tm, D), BF16)],
        compiler_params=_cparams(("parallel", "parallel", "arbitrary")),
        name="in_proj",
    )(x, sh, sc, w_main, w_small, conv_w, conv_b)


def _gla_kernel(q_ref, k_ref, v_ref, sm_ref, wup_ref, bias_ref, s0_ref, o_ref, sout_ref, s_scr,
                *, tb, reverse):
    i = pl.program_id(1)
    L = GLA_CHUNK
    nch = tb // L

    @pl.when(i == 0)
    def _():
        s_scr[...] = s0_ref[0]

    tri = _tri_mask(L, reverse)
    tri_b = jnp.where(tri, 1.0, 0.0).astype(BF16)
    mid = L // 2 if reverse else L // 2 - 1
    last = 0 if reverse else L - 1
    scale = GLA_HK ** -0.5
    wup = wup_ref[0].astype(BF16)
    bias = bias_ref[0]

    order = range(nch - 1, -1, -1) if reverse else range(nch)
    for c in order:
        r0 = c * L
        lo = sm_ref[0, r0:r0 + L, :].astype(BF16)
        g = _log_sigmoid(_dot(lo, wup) + bias) / GLA_NORMALIZER
        G = _tri_cumsum(tri_b, g)
        for h in range(GLA_HEADS):
            ks = slice(h * GLA_HK, (h + 1) * GLA_HK)
            vs = slice(h * GLA_HV, (h + 1) * GLA_HV)
            Gh = G[:, ks]
            Gm = Gh[mid:mid + 1, :]
            Gl = Gh[last:last + 1, :]
            qh = q_ref[0, r0:r0 + L, ks].astype(F32) * scale
            kh = k_ref[0, r0:r0 + L, ks].astype(F32)
            vh = v_ref[0, r0:r0 + L, vs]
            qg = (qh * jnp.exp(Gh - Gm)).astype(BF16)
            kg = (kh * jnp.exp(Gm - Gh)).astype(BF16)
            A = jnp.where(tri, _dot_nt(qg, kg), 0.0)
            st = s_scr[h]
            o = _dot(A.astype(BF16), vh)
            o = o + _dot_nt((qh * jnp.exp(Gh)).astype(BF16), st.astype(BF16))
            o_ref[0, r0:r0 + L, vs] = o.astype(BF16)
            kd = (kh * jnp.exp(Gl - Gh)).astype(BF16)
            s_scr[h] = st * jnp.exp(Gl) + _dot_tn(vh, kd)

    @pl.when(i == pl.num_programs(1) - 1)
    def _():
        sout_ref[0] = s_scr[...]


def _gla_call(hm, hs, wup, bias, s0, *, tb, reverse):
    b, t, _ = hm.shape
    nblk = t // tb

    def tix(i):
        return (nblk - 1 - i) if reverse else i

    d = 1 if reverse else 0
    kern = functools.partial(_gla_kernel, tb=tb, reverse=reverse)
    st_spec = pl.BlockSpec((1, GLA_HEADS, GLA_HV, GLA_HK), lambda bi, i: (bi, 0, 0, 0))
    return pl.pallas_call(
        kern,
        grid=(b, nblk),
        in_specs=[pl.BlockSpec((1, tb, GLA_DK), lambda bi, i: (bi, tix(i), COL_Q)),
                  pl.BlockSpec((1, tb, GLA_DK), lambda bi, i: (bi, tix(i), COL_K)),
                  pl.BlockSpec((1, tb, GLA_DV), lambda bi, i: (bi, tix(i), COL_V)),
                  pl.BlockSpec((1, tb, SMALL_COLS), lambda bi, i: (bi, tix(i), 0)),
                  pl.BlockSpec((1, SMALL_COLS, GLA_DK), lambda bi, i: (d, 0, 0)),
                  pl.BlockSpec((1, 1, GLA_DK), lambda bi, i: (d, 0, 0)),
                  st_spec],
        out_specs=[pl.BlockSpec((1, tb, GLA_DV), lambda bi, i: (bi, tix(i), 0)), st_spec],
        out_shape=[jax.ShapeDtypeStruct((b, t, GLA_DV), BF16),
                   jax.ShapeDtypeStruct((b, GLA_HEADS, GLA_HV, GLA_HK), F32)],
        scratch_shapes=[pltpu.VMEM((GLA_HEADS, GLA_HV, GLA_HK), F32)],
        compiler_params=_cparams(("parallel", "arbitrary")),
        name="gla_bwd" if reverse else "gla_fwd",
    )(hm, hm, hm, hs, wup, bias, s0)


def _ssd_kernel(x_ref, b_ref, c_ref, sm_ref, dtb_ref, alog_ref, s0_ref, y_ref, sout_ref, s_scr,
                *, tb, reverse):
    i = pl.program_id(1)
    L = SSD_CHUNK
    nch = tb // L
    P = SSD_HEADDIM
    GW = SSD_HPG * P

    @pl.when(i == 0)
    def _():
        s_scr[...] = s0_ref[0]

    tri = _tri_mask(L, reverse)
    tri_b = jnp.where(tri, 1.0, 0.0).astype(BF16)
    last = 0 if reverse else L - 1
    cbase = SM_DT[1] if reverse else SM_DT[0]
    dt_bias = dtb_ref[...]
    a_row = -jnp.exp(alog_ref[...])
    col = lax.broadcasted_iota(jnp.int32, (1, SMALL_COLS), 1)
    a_row = jnp.where((col >= cbase) & (col < cbase + SSD_HEADS), a_row, 0.0)
    lane_head = lax.broadcasted_iota(jnp.int32, (1, GW), 1) // P

    def chunk(ci, carry):
        c = (nch - 1 - ci) if reverse else ci
        r0 = pl.multiple_of(c * L, L)
        rows = pl.ds(r0, L)
        dt = _softplus(sm_ref[0, rows, :] + dt_bias)
        cum = _tri_cumsum(tri_b, dt * a_row)
        cum_t = cum.T
        dt_t = dt.T
        cum_last = cum[last:last + 1, :]
        e_in = jnp.exp(cum_last - cum) * dt
        e_out = jnp.exp(cum)
        d_last = jnp.exp(cum_last)
        for g in range(SSD_GROUPS):
            bg = b_ref[0, rows, g * SSD_STATE:(g + 1) * SSD_STATE]
            cg = c_ref[0, rows, g * SSD_STATE:(g + 1) * SSD_STATE]
            xg = x_ref[0, rows, g * GW:(g + 1) * GW]
            xg32 = xg.astype(F32)
            cb = _dot_nt(cg, bg)
            sg = s_scr[g]
            y_int = _dot(cg, sg.astype(BF16))
            ys, xws = [], []
            decay_row = jnp.zeros((1, GW), F32)
            for e in range(SSD_HPG):
                k = cbase + g * SSD_HPG + e
                seg = cum[:, k:k + 1] - cum_t[k:k + 1, :]
                lmat = jnp.exp(jnp.where(tri, seg, -jnp.inf))
                w = (cb * lmat * dt_t[k:k + 1, :]).astype(BF16)
                ps = slice(e * P, (e + 1) * P)
                ys.append(_dot(w, xg[:, ps]) + y_int[:, ps] * e_out[:, k:k + 1])
                xws.append((xg32[:, ps] * e_in[:, k:k + 1]).astype(BF16))
                decay_row = jnp.where(lane_head == e, d_last[:, k:k + 1], decay_row)
            y_ref[0, rows, g * GW:(g + 1) * GW] = jnp.concatenate(ys, axis=1).astype(BF16)
            xw = jnp.concatenate(xws, axis=1)
            s_scr[g] = sg * decay_row + _dot_tn(bg, xw)
        return carry

    lax.fori_loop(0, nch, chunk, 0)

    @pl.when(i == pl.num_programs(1) - 1)
    def _():
        sout_ref[0] = s_scr[...]


def _ssd_call(hm, hs, dtb_row, alog_row, s0, *, tb, reverse):
    b, t, _ = hm.shape
    nblk = t // tb

    def tix(i):
        return (nblk - 1 - i) if reverse else i

    kern = functools.partial(_ssd_kernel, tb=tb, reverse=reverse)
    st_spec = pl.BlockSpec((1, SSD_GROUPS, SSD_STATE, SSD_HPG * SSD_HEADDIM), lambda bi, i: (bi, 0, 0, 0))
    return pl.pallas_call(
        kern,
        grid=(b, nblk),
        in_specs=[pl.BlockSpec((1, tb, SSD_DINNER), lambda bi, i: (bi, tix(i), COL_X)),
                  pl.BlockSpec((1, tb, D), lambda bi, i: (bi, tix(i), COL_B)),
                  pl.BlockSpec((1, tb, D), lambda bi, i: (bi, tix(i), COL_C)),
                  pl.BlockSpec((1, tb, SMALL_COLS), lambda bi, i: (bi, tix(i), 0)),
                  pl.BlockSpec((1, SMALL_COLS), lambda bi, i: (0, 0)),
                  pl.BlockSpec((1, SMALL_COLS), lambda bi, i: (0, 0)),
                  st_spec],
        out_specs=[pl.BlockSpec((1, tb, SSD_DINNER), lambda bi, i: (bi, tix(i), 0)), st_spec],
        out_shape=[jax.ShapeDtypeStruct((b, t, SSD_DINNER), BF16),
                   jax.ShapeDtypeStruct((b, SSD_GROUPS, SSD_STATE, SSD_HPG * SSD_HEADDIM), F32)],
        scratch_shapes=[pltpu.VMEM((SSD_GROUPS, SSD_STATE, SSD_HPG * SSD_HEADDIM), F32)],
        compiler_params=_cparams(("parallel", "arbitrary")),
        name="ssd_bwd" if reverse else "ssd_fwd",
    )(hm, hm, hm, hs, dtb_row, alog_row, s0)


def _mix_out_kernel(x_ref, of_ref, ob_ref, r_ref, yf_ref, yb_ref, xs_ref, z_ref, gt_ref,
                    g1_ref, sh2_ref, sc2_ref, g2_ref, gnw_ref, snw_ref, dsk_ref,
                    wbg_ref, wbs_ref, wo_ref, l1g_ref, l1b_ref, rw_ref, swg_ref, swu_ref, swd_ref,
                    pre_ref, u2_ref, lg_ref):
    o = of_ref[0].astype(F32) + ob_ref[0].astype(F32)
    gnw = gnw_ref[...]
    parts = []
    for h in range(GLA_HEADS):
        oh = o[:, h * GLA_HV:(h + 1) * GLA_HV]
        ms = jnp.mean(oh * oh, axis=-1, keepdims=True)
        parts.append(oh * lax.rsqrt(ms + EPS) * gnw)
    on = jnp.concatenate(parts, axis=1) * _silu(r_ref[0].astype(F32))
    y_gla = _dot(on.astype(BF16), wbg_ref[...])

    y = yf_ref[0].astype(F32) + yb_ref[0].astype(F32) + xs_ref[0].astype(F32) * dsk_ref[...]
    y = y * _silu(z_ref[0].astype(F32))
    gw = SSD_DINNER // SSD_GROUPS
    parts = []
    for g in range(SSD_GROUPS):
        yg = y[:, g * gw:(g + 1) * gw]
        ms = jnp.mean(yg * yg, axis=-1, keepdims=True)
        parts.append(yg * lax.rsqrt(ms + EPS) * snw_ref[:, g * gw:(g + 1) * gw])
    yn = jnp.concatenate(parts, axis=1)
    y_ssd = _dot(yn.astype(BF16), wbs_ref[...])

    gates = jax.nn.sigmoid(gt_ref[0].astype(F32))
    merged = gates[:, :D] * y_gla + gates[:, D:] * y_ssd
    out_l = _dot(merged.astype(BF16), wo_ref[...])

    x1 = _ln(DEEPNORM_ALPHA * x_ref[0] + g1_ref[0] * out_l) * l1g_ref[...] + l1b_ref[...]
    u2 = (_ln(x1) * (1.0 + sc2_ref[0]) + sh2_ref[0]).astype(BF16)
    u2_ref[0] = u2
    lg_ref[0] = _dot(u2, rw_ref[...])
    hmid = _silu(_dot(u2, swg_ref[...])) * _dot(u2, swu_ref[...])
    shared = _dot(hmid.astype(BF16), swd_ref[...])
    pre_ref[0] = DEEPNORM_ALPHA * x1 + g2_ref[0] * shared


def _mix_out_call(x, o_f, o_b, hm, y_f, y_b, g1, sh2, sc2, g2, gnw, snw, dsk,
                  wbg, wbs, wo, l1g, l1b, rw, swg, swu, swd, *, tm):
    b, t, _ = x.shape
    tok = lambda w, cidx: pl.BlockSpec((1, tm, w), lambda bi, i: (bi, i, cidx))
    vec = pl.BlockSpec((1, 1, D), lambda bi, i: (bi, 0, 0))
    full = lambda a: pl.BlockSpec(a.shape, lambda bi, i: (0,) * a.ndim)
    return pl.pallas_call(
        _mix_out_kernel,
        grid=(b, t // tm),
        in_specs=[tok(D, 0), tok(D, 0), tok(D, 0), tok(D, COL_R), tok(SSD_DINNER, 0), tok(SSD_DINNER, 0),
                  tok(SSD_DINNER, COL_X), tok(SSD_DINNER, COL_Z), tok(2 * D, COL_GATES),
                  vec, vec, vec, vec, full(gnw), full(snw), full(dsk),
                  full(wbg), full(wbs), full(wo), full(l1g), full(l1b), full(rw), full(swg), full(swu),
                  full(swd)],
        out_specs=[tok(D, 0), tok(D, 0), tok(N_EXPERTS, 0)],
        out_shape=[jax.ShapeDtypeStruct((b, t, D), F32),
                   jax.ShapeDtypeStruct((b, t, D), BF16),
                   jax.ShapeDtypeStruct((b, t, N_EXPERTS), F32)],
        compiler_params=_cparams(("parallel", "parallel")),
        name="mix_out",
    )(x, o_f, o_b, hm, y_f, y_b, hm, hm, hm, g1, sh2, sc2, g2, gnw, snw, dsk,
      wbg, wbs, wo, l1g, l1b, rw, swg, swu, swd)


def _route_kernel(lg_ref, rb_ref, idx_ref, w_ref, rank_ref, cnt_ref, cnt_scr, *, tm):
    i = pl.program_id(0)

    @pl.when(i == 0)
    def _():
        cnt_scr[...] = jnp.zeros_like(cnt_scr)

    scores = jax.nn.sigmoid(lg_ref[...])
    biased = scores + rb_ref[...]
    neg = -jnp.inf
    lane = lax.broadcasted_iota(jnp.int32, (tm, N_EXPERTS), 1)
    epg = EXPERTS_PER_GROUP
    lane_g = lax.broadcasted_iota(jnp.int32, (tm, epg), 1)

    gs = []
    for g in range(N_EXPERT_GROUPS):
        seg = biased[:, g * epg:(g + 1) * epg]
        m1 = jnp.max(seg, axis=-1, keepdims=True)
        i1 = jnp.min(jnp.where(seg == m1, lane_g, epg), axis=-1, keepdims=True)
        m2 = jnp.max(jnp.where(lane_g == i1, neg, seg), axis=-1, keepdims=True)
        gs.append(m1 + m2)
    masked = jnp.full((tm, N_EXPERTS), neg, F32)
    for g in range(N_EXPERT_GROUPS):
        beat = jnp.zeros((tm, 1), jnp.int32)
        for g2 in range(N_EXPERT_GROUPS):
            if g2 == g:
                continue
            b = (gs[g2] > gs[g]) | ((gs[g2] == gs[g]) & (g2 < g))
            beat = beat + b.astype(jnp.int32)
        keep = beat < TOPK_GROUPS
        in_g = (lane >= g * epg) & (lane < (g + 1) * epg)
        masked = jnp.where(in_g & keep, biased, masked)

    idxs, ws = [], []
    onehot = jnp.zeros((tm, N_EXPERTS), F32)
    for _ in range(TOP_K):
        m = jnp.max(masked, axis=-1, keepdims=True)
        ik = jnp.min(jnp.where(masked == m, lane, N_EXPERTS), axis=-1, keepdims=True)
        sel = lane == ik
        ws.append(jnp.sum(jnp.where(sel, scores, 0.0), axis=-1, keepdims=True))
        idxs.append(ik)
        masked = jnp.where(sel, neg, masked)
        onehot = jnp.where(sel, 1.0, onehot)
    wsum = ws[0]
    for k in range(1, TOP_K):
        wsum = wsum + ws[k]
    inv = ROUTED_SCALE / wsum

    r = lax.broadcasted_iota(jnp.int32, (tm, tm), 0)
    c = lax.broadcasted_iota(jnp.int32, (tm, tm), 1)
    strict = jnp.where(c < r, 1.0, 0.0).astype(BF16)
    prefix = _dot(strict, onehot.astype(BF16)) + cnt_scr[...]
    ranks = [jnp.sum(jnp.where(lane == idxs[k], prefix, 0.0), axis=-1, keepdims=True)
             for k in range(TOP_K)]
    cnt_scr[...] = cnt_scr[...] + jnp.sum(onehot, axis=0, keepdims=True)

    idx_ref[...] = jnp.concatenate(idxs, axis=1)
    w_ref[...] = jnp.concatenate([w * inv for w in ws], axis=1)
    rank_ref[...] = jnp.concatenate(ranks, axis=1).astype(jnp.int32)
    cnt_ref[...] = cnt_scr[...].astype(jnp.int32)


def _route_call(logits, rbias, *, tm):
    nt = logits.shape[0]
    k8 = pl.BlockSpec((tm, TOP_K), lambda i: (i, 0))
    return pl.pallas_call(
        functools.partial(_route_kernel, tm=tm),
        grid=(nt // tm,),
        in_specs=[pl.BlockSpec((tm, N_EXPERTS), lambda i: (i, 0)),
                  pl.BlockSpec((1, N_EXPERTS), lambda i: (0, 0))],
        out_specs=[k8, k8, k8, pl.BlockSpec((1, N_EXPERTS), lambda i: (0, 0))],
        out_shape=[jax.ShapeDtypeStruct((nt, TOP_K), jnp.int32),
                   jax.ShapeDtypeStruct((nt, TOP_K), F32),
                   jax.ShapeDtypeStruct((nt, TOP_K), jnp.int32),
                   jax.ShapeDtypeStruct((1, N_EXPERTS), jnp.int32)],
        scratch_shapes=[pltpu.VMEM((1, N_EXPERTS), F32)],
        compiler_params=_cparams(("arbitrary",)),
        name="moe_route",
    )(logits, rbias)


def _expert_kernel(be_ref, nb_ref, xs_ref, wt_ref, wg_ref, wu_ref, wd_ref, y_ref, wgb, wub, wdb):
    i = pl.program_id(0)
    prev = be_ref[jnp.maximum(i - 1, 0)]
    fresh = (i == 0) | (be_ref[i] != prev)

    @pl.when(fresh)
    def _():
        wgb[...] = wg_ref[0].astype(BF16)
        wub[...] = wu_ref[0].astype(BF16)
        wdb[...] = wd_ref[0].astype(BF16)

    @pl.when(i < nb_ref[0])
    def _():
        x = xs_ref[...]
        hmid = _silu(_dot(x, wgb[...])) * _dot(x, wub[...])
        y = _dot(hmid.astype(BF16), wdb[...])
        y_ref[...] = (y * wt_ref[...]).astype(y_ref.dtype)

    @pl.when(i >= nb_ref[0])
    def _():
        y_ref[...] = jnp.zeros_like(y_ref)


def _expert_call(block_e, n_used, xs, wt, wg, wu, wd):
    n_rows = xs.shape[0]
    n_blocks = n_rows // MOE_BLOCK
    grid_spec = pltpu.PrefetchScalarGridSpec(
        num_scalar_prefetch=2,
        grid=(n_blocks,),
        in_specs=[pl.BlockSpec((MOE_BLOCK, D), lambda i, be, nb: (i, 0)),
                  pl.BlockSpec((MOE_BLOCK, 1), lambda i, be, nb: (i, 0)),
                  pl.BlockSpec((1, D, EXPERT_FF), lambda i, be, nb: (be[i], 0, 0)),
                  pl.BlockSpec((1, D, EXPERT_FF), lambda i, be, nb: (be[i], 0, 0)),
                  pl.BlockSpec((1, EXPERT_FF, D), lambda i, be, nb: (be[i], 0, 0))],
        out_specs=pl.BlockSpec((MOE_BLOCK, D), lambda i, be, nb: (i, 0)),
        scratch_shapes=[pltpu.VMEM((D, EXPERT_FF), BF16), pltpu.VMEM((D, EXPERT_FF), BF16),
                        pltpu.VMEM((EXPERT_FF, D), BF16)],
    )
    return pl.pallas_call(
        _expert_kernel,
        grid_spec=grid_spec,
        out_shape=jax.ShapeDtypeStruct((n_rows, D), BF16),
        compiler_params=_cparams(("arbitrary",)),
        name="moe_experts",
    )(block_e, n_used, xs, wt, wg, wu, wd)


def _final_kernel(pre_ref, acc_ref, g2_ref, lg_ref, lb_ref, o_ref):
    o_ref[0] = _ln(pre_ref[0] + g2_ref[0] * acc_ref[0]) * lg_ref[...] + lb_ref[...]


def _final_call(pre, acc, g2, l2g, l2b, *, tm):
    b, t, _ = pre.shape
    tok = pl.BlockSpec((1, tm, D), lambda bi, i: (bi, i, 0))
    return pl.pallas_call(
        _final_kernel,
        grid=(b, t // tm),
        in_specs=[tok, tok, pl.BlockSpec((1, 1, D), lambda bi, i: (bi, 0, 0)),
                  pl.BlockSpec((1, D), lambda bi, i: (0, 0)), pl.BlockSpec((1, D), lambda bi, i: (0, 0))],
        out_specs=tok,
        out_shape=jax.ShapeDtypeStruct((b, t, D), F32),
        compiler_params=_cparams(("parallel", "parallel")),
        name="final_norm",
    )(pre, acc, g2, l2g, l2b)


def _pack_in_weights(w_in):
    pts, acc = [], 0
    for s in IN_SIZES:
        pts.append((acc, acc + s))
        acc += s
    seg = lambda n: w_in[:, pts[n][0]:pts[n][1]]
    q, k, v, r, lo_f, lo_b, z, xbc, dt_f, dt_b, gates = (seg(n) for n in range(len(IN_SIZES)))
    xs, bm, cm = xbc[:, :SSD_DINNER], xbc[:, SSD_DINNER:SSD_DINNER + D], xbc[:, SSD_DINNER + D:]
    w_main = jnp.concatenate([z, xs, gates, v, r, bm, cm, q, k], axis=1).astype(BF16)
    pad = jnp.zeros((D, SMALL_COLS - 2 * GLA_RANK - 2 * SSD_HEADS), w_in.dtype)
    w_small = jnp.concatenate([lo_f, lo_b, dt_f, dt_b, pad], axis=1).astype(BF16)
    return w_main, w_small


def _small_row(vals, reverse):
    base = SM_DT[1] if reverse else SM_DT[0]
    row = jnp.zeros((SMALL_COLS,), F32).at[base:base + SSD_HEADS].set(vals.astype(F32))
    return row.reshape(1, SMALL_COLS)


def _token_mixer(hm, hs, p, init, *, tb):
    gla0_f, gla0_b, ssd0_f, ssd0_b = init
    o_f, gla_f = _gla_call(hm, hs, p['gla_a_up'], p['gla_a_bias'], gla0_f, tb=tb, reverse=False)
    o_b, gla_b = _gla_call(hm, hs, p['gla_a_up'], p['gla_a_bias'], gla0_b, tb=tb, reverse=True)
    y_f, ssd_f = _ssd_call(hm, hs, p['dtb_f'], p['alog_f'], ssd0_f, tb=tb, reverse=False)
    y_b, ssd_b = _ssd_call(hm, hs, p['dtb_b'], p['alog_b'], ssd0_b, tb=tb, reverse=True)
    return (o_f, o_b, y_f, y_b), (gla_f, gla_b, ssd_f, ssd_b)


def _moe_dispatch_meta(idx, rank, counts):
    counts = counts.reshape(N_EXPERTS)
    padded = (counts + MOE_BLOCK - 1) // MOE_BLOCK * MOE_BLOCK
    pends = jnp.cumsum(padded)
    pstarts = pends - padded
    dest = pstarts[idx] + rank
    n_assign = idx.shape[0] * TOP_K
    n_blocks = -(-n_assign // MOE_BLOCK) + N_EXPERTS
    block_start = jnp.arange(n_blocks, dtype=jnp.int32) * MOE_BLOCK
    block_e = jnp.minimum(jnp.searchsorted(pends, block_start, side='right'), N_EXPERTS - 1)
    n_used = (pends[-1] // MOE_BLOCK).astype(jnp.int32).reshape(1)
    return dest.astype(jnp.int32), block_e.astype(jnp.int32), n_used, n_blocks


def kernel(x, c, ctx, c_ctx, w_mod, b_mod, w_in, gla_a_up, gla_a_bias, gla_norm_w, ssd_conv_w, ssd_conv_b, ssd_dt_bias, ssd_a_log, ssd_d, ssd_norm_w, w_branch_gla, w_branch_ssd, w_out, ln1_g, ln1_b, ln2_g, ln2_b, router_w, router_bias, exp_w_gate, exp_w_up, exp_w_down, sh_w_gate, sh_w_up, sh_w_down):
    bsz, t, _ = x.shape
    tc = ctx.shape[1]
    l = 0

    cc = jnp.concatenate([c, c_ctx[None, :], jnp.zeros((8 - bsz - 1, D), c.dtype)], axis=0)
    mod = _mod_call(cc, w_mod[l], b_mod[l])
    mod_l = mod[:bsz].reshape(bsz, 6, 1, D)
    sh1_l, sc1_l, g1_l, sh2_l, sc2_l, g2_l = (mod_l[:, n] for n in range(6))
    mod_c = jnp.broadcast_to(mod[bsz].reshape(1, 6, 1, D), (bsz, 6, 1, D))
    sh1_c, sc1_c = mod_c[:, 0], mod_c[:, 1]

    w_main, w_small = _pack_in_weights(w_in[l])
    conv_w = ssd_conv_w[l]
    conv_b = ssd_conv_b[l].reshape(1, SSD_XBC)
    a_up = jnp.zeros((2, SMALL_COLS, GLA_DK), F32)
    a_up = a_up.at[0, SM_LO[0]:SM_LO[0] + GLA_RANK].set(gla_a_up[l, 0])
    a_up = a_up.at[1, SM_LO[1]:SM_LO[1] + GLA_RANK].set(gla_a_up[l, 1])
    p = {
        'gla_a_up': a_up, 'gla_a_bias': gla_a_bias[l].reshape(2, 1, GLA_DK),
        'dtb_f': _small_row(ssd_dt_bias[l, 0], False), 'dtb_b': _small_row(ssd_dt_bias[l, 1], True),
        'alog_f': _small_row(ssd_a_log[l, 0], False), 'alog_b': _small_row(ssd_a_log[l, 1], True),
    }

    hm_c, hs_c = _inproj_call(ctx, sh1_c, sc1_c, w_main, w_small, conv_w, conv_b, tm=tc, conv_rows=tc)
    zg = jnp.zeros((bsz, GLA_HEADS, GLA_HV, GLA_HK), F32)
    zs = jnp.zeros((bsz, SSD_GROUPS, SSD_STATE, SSD_HPG * SSD_HEADDIM), F32)
    _, ctx_states = _token_mixer(hm_c, hs_c, p, (zg, zg, zs, zs), tb=tc)

    hm, hs = _inproj_call(x, sh1_l, sc1_l, w_main, w_small, conv_w, conv_b, tm=min(512, t),
                          conv_rows=GRID_W)
    (o_f, o_b, y_f, y_b), _ = _token_mixer(hm, hs, p, ctx_states, tb=min(512, t))

    dsk = jnp.repeat(ssd_d[l], SSD_HEADDIM).reshape(1, SSD_DINNER)
    pre, u2, logits = _mix_out_call(
        x, o_f, o_b, hm, y_f, y_b, g1_l, sh2_l, sc2_l, g2_l,
        gla_norm_w[l].reshape(1, GLA_HV), ssd_norm_w[l].reshape(1, SSD_DINNER), dsk,
        w_branch_gla[l].astype(BF16), w_branch_ssd[l].astype(BF16), w_out[l].astype(BF16),
        ln1_g[l].reshape(1, D), ln1_b[l].reshape(1, D), router_w[l].astype(BF16),
        sh_w_gate[l].astype(BF16), sh_w_up[l].astype(BF16), sh_w_down[l].astype(BF16), tm=min(256, t))

    nt = bsz * t
    idx, wts, rank, counts = _route_call(logits.reshape(nt, N_EXPERTS), router_bias[l].reshape(1, N_EXPERTS),
                                         tm=min(256, nt))
    dest, block_e, n_used, n_blocks = _moe_dispatch_meta(idx, rank, counts)
    n_rows = n_blocks * MOE_BLOCK
    flat_dest = dest.reshape(-1)
    flat_tok = jnp.repeat(jnp.arange(nt, dtype=jnp.int32), TOP_K)
    buf_tok = jnp.full((n_rows,), nt, jnp.int32).at[flat_dest].set(flat_tok)
    buf_w = jnp.zeros((n_rows,), F32).at[flat_dest].set(wts.reshape(-1))
    u2_pad = jnp.concatenate([u2.reshape(nt, D), jnp.zeros((1, D), BF16)], axis=0)
    xs_sorted = u2_pad[buf_tok]
    y_sorted = _expert_call(block_e, n_used, xs_sorted, buf_w.reshape(n_rows, 1),
                            exp_w_gate[l], exp_w_up[l], exp_w_down[l])
    acc = jnp.sum(y_sorted[dest].astype(F32), axis=1).reshape(bsz, t, D)

    return _final_call(pre, acc, g2_l, ln2_g[l].reshape(1, D), ln2_b[l].reshape(1, D), tm=min(512, t))
```
